```python
import functools
import jax, jax.numpy as jnp
from jax import lax
import numpy as np

D_MODEL = 1024
BATCH = 4
SEQ = 8192
DEPTH = 1
DEC_BATCH = 128
DEC_SEQ = 1
PAST_LEN = 16384
PAGE_SIZE = 128

MLA_HEADS = 8
MLA_NOPE = 64
MLA_ROPE = 32
MLA_QK = MLA_NOPE + MLA_ROPE
MLA_V = 64
Q_RANK = 384
KV_RANK = 256
FOX_HEADS = 8
FOX_DIM = 64
MLA_WIDTH = MLA_HEADS * MLA_V
FOX_WIDTH = FOX_HEADS * FOX_DIM
MIX_WIDTH = MLA_WIDTH + FOX_WIDTH
IN_WIDTH = Q_RANK + KV_RANK + MLA_ROPE + 3 * FOX_WIDTH + FOX_HEADS
IN_SPLITS = (Q_RANK,
             Q_RANK + KV_RANK,
             Q_RANK + KV_RANK + MLA_ROPE,
             Q_RANK + KV_RANK + MLA_ROPE + FOX_WIDTH,
             Q_RANK + KV_RANK + MLA_ROPE + 2 * FOX_WIDTH,
             Q_RANK + KV_RANK + MLA_ROPE + 3 * FOX_WIDTH)
D_FF = 2816
N_MOD = 9
ROPE_THETA = 10000.0
EPS = 1e-6
Q_BLOCK = 128
NEG = -1e30

kernel_name = "hymba_mla_fox_macaron_adaln_step"


def rms_norm(x, g):
    xf = x.astype(jnp.float32)
    y = xf * lax.rsqrt(jnp.mean(xf * xf, axis=-1, keepdims=True) + EPS)
    return (y * g.astype(jnp.float32)).astype(x.dtype)


def rope(x, pos):
    half = MLA_ROPE // 2
    inv = ROPE_THETA ** (-jnp.arange(half, dtype=jnp.float32) / half)
    ang = pos.astype(jnp.float32)[:, None] * inv[None, :]
    cos, sin = jnp.cos(ang)[:, None, :], jnp.sin(ang)[:, None, :]
    x1, x2 = x[..., :half].astype(jnp.float32), x[..., half:].astype(jnp.float32)
    return jnp.concatenate([x1 * cos - x2 * sin, x1 * sin + x2 * cos], axis=-1).astype(x.dtype)


def swiglu(h, w_gate, w_up, w_down):
    return (jax.nn.silu(h @ w_gate) * (h @ w_up)) @ w_down


def mixer_projections(h, pos, w_in, g_q_a, w_q_b, g_kv_a, g_mla_q, b_fox_f, g_fox_q, g_fox_k):
    B, S, _ = h.shape
    z = h @ w_in
    q_lat, kv_lat, k_pe, fq, fk, fv, f_logit = jnp.split(z, IN_SPLITS, axis=-1)
    q = jnp.einsum('bsr,rhd->bshd', rms_norm(q_lat, g_q_a), w_q_b)
    q = rms_norm(q, g_mla_q)
    q = jnp.concatenate([q[..., :MLA_NOPE], rope(q[..., MLA_NOPE:], pos)], axis=-1)
    c_kv = rms_norm(kv_lat, g_kv_a)
    fq = rms_norm(fq.reshape(B, S, FOX_HEADS, FOX_DIM), g_fox_q)
    fk = rms_norm(fk.reshape(B, S, FOX_HEADS, FOX_DIM), g_fox_k)
    fv = fv.reshape(B, S, FOX_HEADS, FOX_DIM)
    logf = jax.nn.log_sigmoid((f_logit + b_fox_f).astype(jnp.float32))
    return q, c_kv, k_pe, fq, fk, fv, logf


def mla_keys(c_kv, k_pe, pos, w_kv_b, g_mla_k):
    kv = jnp.einsum('bsr,rhd->bshd', c_kv, w_kv_b)
    k_nope, v = kv[..., :MLA_NOPE], kv[..., MLA_NOPE:]
    k_pe_h = jnp.broadcast_to(k_pe[:, :, None, :], k_nope.shape[:-1] + (MLA_ROPE,)).astype(k_nope.dtype)
    k = rms_norm(jnp.concatenate([k_nope, k_pe_h], axis=-1), g_mla_k)
    k = jnp.concatenate([k[..., :MLA_NOPE], rope(k[..., MLA_NOPE:], pos)], axis=-1)
    return k, v


def block_causal_attention(q, k, v, scale, F=None):
    B, S, H, Dq = q.shape
    nb = S // Q_BLOCK
    qb = q.reshape(B, nb, Q_BLOCK, H, Dq).swapaxes(0, 1)
    Fb = None if F is None else F.reshape(B, nb, Q_BLOCK, H).swapaxes(0, 1)
    Fk = None if F is None else F.transpose(0, 2, 1)[:, :, None, :]
    kpos = jnp.arange(S)

    def one(xs):
        i, qi, Fi = xs
        logits = jnp.einsum('bqhd,bkhd->bhqk', qi, k).astype(jnp.float32) * scale
        if Fi is not None:
            logits = logits + Fi.transpose(0, 2, 1)[..., None] - Fk
        qpos = i * Q_BLOCK + jnp.arange(Q_BLOCK)
        logits = jnp.where(kpos[None, :] <= qpos[:, None], logits, NEG)
        p = jax.nn.softmax(logits, axis=-1)
        return jnp.einsum('bhqk,bkhd->bqhd', p.astype(v.dtype), v)

    out = lax.map(one, (jnp.arange(nb), qb, Fb))
    return out.swapaxes(0, 1).reshape(B, S, H, v.shape[-1])


def init_carry(B, H, T, D):
    return (jnp.full((B, H, T), NEG, jnp.float32), jnp.zeros((B, H, T), jnp.float32),
            jnp.zeros((B, H, T, D), jnp.float32))


def online_update(carry, logits, v):
    m, l, acc = carry
    m_new = jnp.maximum(m, jnp.max(logits, axis=-1))
    alpha = jnp.exp(m - m_new)
    p = jnp.exp(logits - m_new[..., None])
    l = l * alpha + jnp.sum(p, axis=-1)
    acc = acc * alpha[..., None] + jnp.einsum('bhtp,bphd->bhtd', p, v.astype(jnp.float32))
    return (m_new, l, acc)


def finalize(carry, dtype):
    _, l, acc = carry
    return (acc / l[..., None]).transpose(0, 2, 1, 3).astype(dtype)


def mla_decode(q, c_new, kpe_new, pos_new, cache_ckv, cache_kpe, layer, page_table, w_kv_b, g_mla_k):
    DB, T = q.shape[:2]
    scale = MLA_QK ** -0.5

    def body(carry, xs):
        p, pages = xs
        pos = p * PAGE_SIZE + jnp.arange(PAGE_SIZE)
        k, v = mla_keys(cache_ckv[layer, pages], cache_kpe[layer, pages], pos, w_kv_b, g_mla_k)
        logits = jnp.einsum('bthd,bphd->bhtp', q, k).astype(jnp.float32) * scale
        return online_update(carry, logits, v), None

    n_pages = page_table.shape[1]
    carry, _ = lax.scan(body, init_carry(DB, MLA_HEADS, T, MLA_V), (jnp.arange(n_pages), page_table.T))
    k_new, v_new = mla_keys(c_new, kpe_new, pos_new, w_kv_b, g_mla_k)
    logits = jnp.einsum('bthd,bshd->bhts', q, k_new).astype(jnp.float32) * scale
    logits = jnp.where(jnp.tril(jnp.ones((T, T), bool)), logits, NEG)
    carry = online_update(carry, logits, v_new)
    return finalize(carry, v_new.dtype)


def fox_decode(q, k_new, v_new, logf_new, cache_k, cache_v, cache_logf, layer, page_table):
    DB, T, H, _ = q.shape
    n_pages = page_table.shape[1]
    scale = FOX_DIM ** -0.5
    logf_past = cache_logf[layer, page_table].astype(jnp.float32).reshape(DB, n_pages * PAGE_SIZE, H)
    F_past = jnp.cumsum(logf_past, axis=1)
    F_q = F_past[:, -1:, :] + jnp.cumsum(logf_new, axis=1)
    Fq = F_q.transpose(0, 2, 1)[..., None]
    F_pages = F_past.reshape(DB, n_pages, PAGE_SIZE, H).transpose(1, 0, 3, 2)

    def body(carry, xs):
        pages, F_p = xs
        logits = jnp.einsum('bthd,bphd->bhtp', q, cache_k[layer, pages]).astype(jnp.float32) * scale
        logits = logits + Fq - F_p[:, :, None, :]
        return online_update(carry, logits, cache_v[layer, pages]), None

    carry, _ = lax.scan(body, init_carry(DB, H, T, FOX_DIM), (page_table.T, F_pages))
    logits = jnp.einsum('bthd,bshd->bhts', q, k_new).astype(jnp.float32) * scale
    logits = logits + Fq - F_q.transpose(0, 2, 1)[:, :, None, :]
    logits = jnp.where(jnp.tril(jnp.ones((T, T), bool)), logits, NEG)
    carry = online_update(carry, logits, v_new)
    return finalize(carry, v_new.dtype)


def merge_groups(o_mla, o_fox, g_out_mla, g_out_fox, w_o):
    B, S = o_mla.shape[:2]
    o = jnp.concatenate([rms_norm(o_mla.reshape(B, S, MLA_WIDTH), g_out_mla),
                         rms_norm(o_fox.reshape(B, S, FOX_WIDTH), g_out_fox)], axis=-1)
    return o @ w_o


def prompt_mixer(h, pos, w_in, g_q_a, w_q_b, g_kv_a, w_kv_b, g_mla_q, g_mla_k, b_fox_f,
                 g_fox_q, g_fox_k, g_out_mla, g_out_fox, w_o):
    q, c_kv, k_pe, fq, fk, fv, logf = mixer_projections(h, pos, w_in, g_q_a, w_q_b, g_kv_a, g_mla_q,
                                                        b_fox_f, g_fox_q, g_fox_k)
    k, v = mla_keys(c_kv, k_pe, pos, w_kv_b, g_mla_k)
    o_mla = block_causal_attention(q, k, v, MLA_QK ** -0.5)
    o_fox = block_causal_attention(fq, fk, fv, FOX_DIM ** -0.5, jnp.cumsum(logf, axis=1))
    return merge_groups(o_mla, o_fox, g_out_mla, g_out_fox, w_o), (c_kv, k_pe, fk, fv, logf)


def sample_mixer(h, pos, layer, cache_mla_ckv, cache_mla_kpe, cache_fox_k, cache_fox_v, cache_fox_logf,
                 page_table, w_in, g_q_a, w_q_b, g_kv_a, w_kv_b, g_mla_q, g_mla_k, b_fox_f,
                 g_fox_q, g_fox_k, g_out_mla, g_out_fox, w_o):
    q, c_kv, k_pe, fq, fk, fv, logf = mixer_projections(h, pos, w_in, g_q_a, w_q_b, g_kv_a, g_mla_q,
                                                        b_fox_f, g_fox_q, g_fox_k)
    o_mla = mla_decode(q, c_kv, k_pe, pos, cache_mla_ckv, cache_mla_kpe, layer, page_table, w_kv_b, g_mla_k)
    o_fox = fox_decode(fq, fk, fv, logf, cache_fox_k, cache_fox_v, cache_fox_logf, layer, page_table)
    return merge_groups(o_mla, o_fox, g_out_mla, g_out_fox, w_o), (c_kv, k_pe, fk, fv, logf)


def trunk_layer(x, c, mixer, w_ada, b_ada, g_ffn1, w1_gate, w1_up, w1_down, g_mix,
                g_ffn2, w2_gate, w2_up, w2_down):
    mod = jax.nn.silu(c) @ w_ada + b_ada
    sh1, sc1, gt1, sh2, sc2, gt2, sh3, sc3, gt3 = jnp.split(mod[:, None, :], N_MOD, axis=-1)
    h = rms_norm(x, g_ffn1) * (1 + sc1) + sh1
    x = x + 0.5 * gt1 * swiglu(h, w1_gate, w1_up, w1_down)
    h = rms_norm(x, g_mix) * (1 + sc2) + sh2
    mix, state = mixer(h)
    x = x + gt2 * mix
    h = rms_norm(x, g_ffn2) * (1 + sc3) + sh3
    x = x + 0.5 * gt3 * swiglu(h, w2_gate, w2_up, w2_down)
    return x, state


def setup_inputs(seed: int = 0) -> dict:
    key = jax.random.key(seed)
    ks = iter(jax.random.split(key, 64))

    def nrm(shape, scale):
        return jax.random.normal(next(ks), shape, jnp.float32) * scale

    def gain(shape):
        return 1.0 + nrm(shape, 0.05)

    n_pages = PAST_LEN // PAGE_SIZE
    n_used = DEC_BATCH * n_pages
    n_pool = n_used + max(1, n_used // 4)
    perm = jax.random.permutation(next(ks), n_pool)
    page_table = perm[:n_used].reshape(DEC_BATCH, n_pages).astype(jnp.int32)
    L = DEPTH
    return {
        "x_prompt": nrm((BATCH, SEQ, D_MODEL), 1.0),
        "x_sample": nrm((DEC_BATCH, DEC_SEQ, D_MODEL), 1.0),
        "cache_mla_ckv": nrm((L, n_pool, PAGE_SIZE, KV_RANK), 1.0),
        "cache_mla_kpe": nrm((L, n_pool, PAGE_SIZE, MLA_ROPE), 1.0),
        "cache_fox_k": nrm((L, n_pool, PAGE_SIZE, FOX_HEADS, FOX_DIM), 1.0),
        "cache_fox_v": nrm((L, n_pool, PAGE_SIZE, FOX_HEADS, FOX_DIM), 1.0),
        "cache_fox_logf": jax.nn.log_sigmoid(nrm((L, n_pool, PAGE_SIZE, FOX_HEADS), 1.0) + 3.5),
        "page_table": page_table,
        "c_prompt": nrm((BATCH, D_MODEL), 1.0),
        "c_sample": nrm((DEC_BATCH, D_MODEL), 1.0),
        "w_ada": nrm((L, D_MODEL, N_MOD * D_MODEL), 0.5 * D_MODEL ** -0.5),
        "b_ada": nrm((L, N_MOD * D_MODEL), 0.02),
        "g_ffn1": gain((L, D_MODEL)),
        "w1_gate": nrm((L, D_MODEL, D_FF), D_MODEL ** -0.5),
        "w1_up": nrm((L, D_MODEL, D_FF), D_MODEL ** -0.5),
        "w1_down": nrm((L, D_FF, D_MODEL), D_FF ** -0.5),
        "g_mix": gain((L, D_MODEL)),
        "w_in": nrm((L, D_MODEL, IN_WIDTH), D_MODEL ** -0.5),
        "g_q_a": gain((L, Q_RANK)),
        "w_q_b": nrm((L, Q_RANK, MLA_HEADS, MLA_QK), Q_RANK ** -0.5),
        "g_kv_a": gain((L, KV_RANK)),
        "w_kv_b": nrm((L, KV_RANK, MLA_HEADS, MLA_NOPE + MLA_V), KV_RANK ** -0.5),
        "g_mla_q": gain((L, MLA_QK)),
        "g_mla_k": gain((L, MLA_QK)),
        "b_fox_f": 1.0 + 5.0 * jax.random.uniform(next(ks), (L, FOX_HEADS), jnp.float32),
        "g_fox_q": gain((L, FOX_DIM)),
        "g_fox_k": gain((L, FOX_DIM)),
        "g_out_mla": gain((L, MLA_WIDTH)),
        "g_out_fox": gain((L, FOX_WIDTH)),
        "w_o": nrm((L, MIX_WIDTH, D_MODEL), MIX_WIDTH ** -0.5),
        "g_ffn2": gain((L, D_MODEL)),
        "w2_gate": nrm((L, D_MODEL, D_FF), D_MODEL ** -0.5),
        "w2_up": nrm((L, D_MODEL, D_FF), D_MODEL ** -0.5),
        "w2_down": nrm((L, D_FF, D_MODEL), D_FF ** -0.5),
    }


def reference(x_prompt, x_sample, cache_mla_ckv, cache_mla_kpe, cache_fox_k, cache_fox_v, cache_fox_logf,
              page_table, c_prompt, c_sample, w_ada, b_ada, g_ffn1, w1_gate, w1_up, w1_down, g_mix,
              w_in, g_q_a, w_q_b, g_kv_a, w_kv_b, g_mla_q, g_mla_k, b_fox_f, g_fox_q, g_fox_k,
              g_out_mla, g_out_fox, w_o, g_ffn2, w2_gate, w2_up, w2_down):
    pos_prompt = jnp.arange(x_prompt.shape[1])
    pos_sample = PAST_LEN + jnp.arange(x_sample.shape[1])
    yp, ys = x_prompt, x_sample
    prompt_states, sample_states = [], []
    for l in range(DEPTH):
        mw = (w_in[l], g_q_a[l], w_q_b[l], g_kv_a[l], w_kv_b[l], g_mla_q[l], g_mla_k[l], b_fox_f[l],
              g_fox_q[l], g_fox_k[l], g_out_mla[l], g_out_fox[l], w_o[l])
        lw = (w_ada[l], b_ada[l], g_ffn1[l], w1_gate[l], w1_up[l], w1_down[l], g_mix[l],
              g_ffn2[l], w2_gate[l], w2_up[l], w2_down[l])
        p_mixer = lambda h, mw=mw: prompt_mixer(h, pos_prompt, *mw)
        s_mixer = lambda h, mw=mw, l=l: sample_mixer(h, pos_sample, l, cache_mla_ckv, cache_mla_kpe,
                                                     cache_fox_k, cache_fox_v, cache_fox_logf,
                                                     page_table, *mw)
        yp, sp = trunk_layer(yp, c_prompt, p_mixer, *lw)
        ys, ss = trunk_layer(ys, c_sample, s_mixer, *lw)
        prompt_states.append(sp)
        sample_states.append(ss)
    p_ckv, p_kpe, p_fk, p_fv, p_logf = [jnp.stack(z) for z in zip(*prompt_states)]
    s_ckv, s_kpe, s_fk, s_fv, s_logf = [jnp.stack(z) for z in zip(*sample_states)]
    return (yp, ys, p_ckv, p_kpe, p_fk, p_fv, p_logf, s_ckv, s_kpe, s_fk, s_fv, s_logf)
```

```python
import functools
import math

import jax
import jax.numpy as jnp
from jax import lax
from jax.experimental import pallas as pl
from jax.experimental.pallas import tpu as pltpu

F32 = jnp.float32
BF16 = jnp.bfloat16
EPS = 1e-6
ROPE_THETA = 10000.0
NEG = -1e30
LOG2E = 1.4426950408889634
LANES = 128
SUBLANES = 8
MXU_DIM = 256
VMEM_LIMIT_BYTES = 56 * 1024 * 1024

_NT = (((1,), (1,)), ((), ()))
_TN = (((0,), (0,)), ((), ()))


def _params(sem):
    return pltpu.CompilerParams(dimension_semantics=sem, vmem_limit_bytes=VMEM_LIMIT_BYTES)


def _resident(shape):
    nd = len(shape)
    return pl.BlockSpec(shape, lambda *_: (0,) * nd, pipeline_mode=pl.Buffered(1))


def _dot(a, b):
    return jnp.dot(a, b, preferred_element_type=F32)


def _dot_nt(a, b):
    return lax.dot_general(a, b, _NT, preferred_element_type=F32)


def _rms(x, g):
    return x * lax.rsqrt(jnp.mean(x * x, axis=-1, keepdims=True) + EPS) * g


def _split3(x):
    hi = x.astype(BF16)
    r1 = x - hi.astype(F32)
    mid = r1.astype(BF16)
    lo = (r1 - mid.astype(F32)).astype(BF16)
    return hi, mid, lo


def _log_sigmoid(x):
    return jnp.minimum(x, 0.0) - jnp.log1p(jnp.exp(-jnp.abs(x)))


def _adaln_kernel(c_ref, w_ref, b_ref, o_ref):
    c = c_ref[...]
    a = (c * jax.nn.sigmoid(c)).astype(BF16)
    o_ref[...] = _dot(a, w_ref[...].astype(BF16)) + b_ref[...]


def _adaln(c, w, b):
    rows, d = c.shape
    n = w.shape[1]
    tn = n // 8 if (n // 8) % LANES == 0 else n
    return pl.pallas_call(
        _adaln_kernel,
        grid=(n // tn,),
        in_specs=[pl.BlockSpec((rows, d), lambda j: (0, 0)),
                  pl.BlockSpec((d, tn), lambda j: (0, j)),
                  pl.BlockSpec((1, tn), lambda j: (0, j))],
        out_specs=pl.BlockSpec((rows, tn), lambda j: (0, j)),
        out_shape=jax.ShapeDtypeStruct((rows, n), F32),
        compiler_params=_params(("arbitrary",)),
        name="adaln",
    )(c, w, b.reshape(1, n))


def _rope_kernel(inv_ref, cos_ref, sin_ref):
    tp = cos_ref.shape[0]
    pos = (pl.program_id(0) * tp + lax.broadcasted_iota(jnp.int32, (tp, LANES), 0)).astype(F32)
    ang = pos * inv_ref[...]
    cos_ref[...] = jnp.cos(ang)
    sin_ref[...] = jnp.sin(ang)


def _rope_tables(n_pos, half):
    tp = 512
    n_pad = -(-n_pos // tp) * tp
    inv = ROPE_THETA ** (-jnp.arange(half, dtype=F32) / half)
    inv_l = jnp.tile(inv, LANES // half).reshape(1, LANES)
    return pl.pallas_call(
        _rope_kernel,
        grid=(n_pad // tp,),
        in_specs=[pl.BlockSpec((1, LANES), lambda i: (0, 0))],
        out_specs=[pl.BlockSpec((tp, LANES), lambda i: (i, 0))] * 2,
        out_shape=[jax.ShapeDtypeStruct((n_pad, LANES), F32)] * 2,
        compiler_params=_params(("arbitrary",)),
        name="rope_tables",
    )(inv_l)


def _ffn_core(x, sh, sc, gt, g, wg_ref, wu_ref, wd_ref):
    h = (_rms(x, g) * (1.0 + sc) + sh).astype(BF16)
    acc = jnp.zeros(x.shape, F32)
    for c in range(wg_ref.shape[0]):
        gate = _dot(h, wg_ref[c])
        up = _dot(h, wu_ref[c])
        act = (gate * jax.nn.sigmoid(gate) * up).astype(BF16)
        acc = acc + _dot(act, wd_ref[c])
    return x + (0.5 * gt) * acc


def _ffn_kernel(x_ref, sh_ref, sc_ref, gt_ref, g_ref, wg_ref, wu_ref, wd_ref, o_ref):
    o_ref[0] = _ffn_core(x_ref[0], sh_ref[0], sc_ref[0], gt_ref[0], g_ref[...], wg_ref, wu_ref, wd_ref)


def _mod_spec(mod, tm, d):
    if mod.shape[1] == 1:
        return pl.BlockSpec((1, 1, d), lambda b, i: (b, 0, 0))
    return pl.BlockSpec((1, tm, d), lambda b, i: (b, i, 0))


def _ffn(x, sh, sc, gt, g, wg, wu, wd, tm):
    bsz, s, d = x.shape
    tok = pl.BlockSpec((1, tm, d), lambda b, i: (b, i, 0))
    return pl.pallas_call(
        _ffn_kernel,
        grid=(bsz, s // tm),
        in_specs=[tok, _mod_spec(sh, tm, d), _mod_spec(sc, tm, d), _mod_spec(gt, tm, d),
                  _resident(g.shape), _resident(wg.shape), _resident(wu.shape), _resident(wd.shape)],
        out_specs=tok,
        out_shape=jax.ShapeDtypeStruct(x.shape, F32),
        compiler_params=_params(("arbitrary", "arbitrary")),
        name="ffn",
    )(x, sh, sc, gt, g, wg, wu, wd)


def _rope_cs(cos, sin, pe_off, rope):
    lane = lax.broadcasted_iota(jnp.int32, cos.shape, 1)
    pe = (lane >= pe_off) & (lane < pe_off + rope)
    return jnp.where(pe, cos, 1.0), jnp.where(pe, sin, 0.0)


def _proj_prompt_kernel(x_ref, sh_ref, sc_ref, g_ref, win_ref, gqa_ref, wq_ref, wqr_ref, gkva_ref, wk_ref,
                        wvt_ref, wfvt_ref, gq_ref, gqr_ref, gk_ref, gkr_ref, gfq_ref, gfk_ref, bf_ref,
                        cos_ref, sin_ref, tri_ref, place_ref, qones_ref, kones_ref,
                        qm_ref, km_ref, vtm_ref, qf_ref, kf_ref, vtf_ref, ckv_ref, kpe_ref, fk_ref, fv_ref,
                        logf_ref, carry_ref, *, dm):
    q_rank, kv_rank, heads, fheads = dm["q_rank"], dm["kv_rank"], dm["heads"], dm["fheads"]
    rope, qk, fdim = dm["rope"], dm["qk"], dm["fdim"]
    tm = x_ref.shape[1]
    h = (_rms(x_ref[0], g_ref[...]) * (1.0 + sc_ref[0]) + sh_ref[0]).astype(BF16)

    off = [0]

    def section(width):
        z = _dot(h, win_ref[:, off[0]:off[0] + width])
        off[0] += width
        return z

    z_q = section(q_rank)
    z_kv = section(kv_rank)
    z_pe = section(2 * LANES)
    c_tab, s_tab = _rope_cs(cos_ref[...], sin_ref[...], dm["pe_off"], rope)

    qn = _rms(z_q, gqa_ref[...]).astype(BF16)
    zq = _dot(qn, wq_ref[...])
    zqr = _dot(qn, wqr_ref[...])
    gq, gqr = gq_ref[...], gqr_ref[...]
    qscale = qk ** -0.5 * LOG2E
    for hh in range(heads):
        sl = slice(hh * LANES, (hh + 1) * LANES)
        a, b = zq[:, sl], zqr[:, sl]
        r = lax.rsqrt(jnp.sum(a * a, axis=-1, keepdims=True) * (1.0 / qk) + EPS)
        qm_ref[0, :, sl] = (((a * gq) * c_tab + (b * gqr) * s_tab) * (r * qscale)).astype(BF16)

    ckv = _rms(z_kv, gkva_ref[...])
    ckv_ref[0] = ckv
    ckv_b = ckv.astype(BF16)
    zk = _dot(ckv_b, wk_ref[...])
    vtm_ref[0, 0] = _dot_nt(wvt_ref[...], ckv_b).astype(BF16)
    kpe_a, kpe_b = z_pe[:, :LANES], z_pe[:, LANES:]
    kpe_ref[0] = kpe_b[:, :rope]
    gk, gkr = gk_ref[...], gkr_ref[...]
    for hh in range(heads):
        sl = slice(hh * LANES, (hh + 1) * LANES)
        kfull = zk[:, sl] + kpe_a
        r = lax.rsqrt(jnp.sum(kfull * kfull, axis=-1, keepdims=True) * (1.0 / qk) + EPS)
        km_ref[0, :, sl] = (((kfull * gk) * c_tab + (kpe_b * gkr) * s_tab) * r).astype(BF16)

    z_fq = section(fheads * LANES)
    z_fk = section(fheads * LANES)
    z_fv = section(fheads * fdim)
    z_f = section(LANES) + bf_ref[...]
    logf = _log_sigmoid(z_f)
    logf_ref[0] = logf[:, :fheads]

    @pl.when(pl.program_id(1) == 0)
    def _():
        carry_ref[...] = jnp.zeros_like(carry_ref)

    tri = tri_ref[...]
    l_hi, l_mid, l_lo = _split3(logf)
    fcum = _dot(tri, l_hi) + _dot(tri, l_mid) + _dot(tri, l_lo) + carry_ref[...]
    carry_ref[...] = fcum[tm - 1:tm, :]
    f_hi, f_mid, f_lo = _split3(fcum * LOG2E)
    lane = lax.broadcasted_iota(jnp.int32, (tm, LANES), 1)
    parts = jnp.where(lane < fheads, f_hi, jnp.where(lane < 2 * fheads, f_mid, f_lo))
    placed = _dot(parts, place_ref[...])
    q_add = placed[:, :fheads * LANES] + qones_ref[...]
    k_add = placed[:, fheads * LANES:] + kones_ref[...]

    gfq, gfk = gfq_ref[...], gfk_ref[...]
    fscale = fdim ** -0.5 * LOG2E
    for hh in range(fheads):
        sl = slice(hh * LANES, (hh + 1) * LANES)
        a = z_fq[:, sl]
        r = lax.rsqrt(jnp.sum(a * a, axis=-1, keepdims=True) * (1.0 / fdim) + EPS)
        qf_ref[0, :, sl] = ((a * gfq) * (r * fscale) + q_add[:, sl]).astype(BF16)
        a = z_fk[:, sl]
        r = lax.rsqrt(jnp.sum(a * a, axis=-1, keepdims=True) * (1.0 / fdim) + EPS)
        kn = (a * gfk) * r
        fk_ref[0, :, sl] = kn
        kf_ref[0, :, sl] = (kn + k_add[:, sl]).astype(BF16)
    fv_ref[0] = z_fv
    vtf_ref[0, 0] = _dot_nt(wfvt_ref[...], h).astype(BF16)


def _proj_sample_kernel(x_ref, sh_ref, sc_ref, g_ref, win_ref, gqa_ref, wq_ref, wqr_ref, gkva_ref,
                        gq_ref, gqr_ref, gkn_ref, gkp_ref, gfq_ref, gfk_ref, bf_ref, cos_ref, sin_ref, bdk_ref,
                        qabs_ref, qc_ref, qs_ref, cnew_ref, kpe_ref, fq_ref, fk_ref, fv_ref, logf_ref, *, dm):
    q_rank, kv_rank, heads, fheads = dm["q_rank"], dm["kv_rank"], dm["heads"], dm["fheads"]
    rope, qk, fdim = dm["rope"], dm["qk"], dm["fdim"]
    h = (_rms(x_ref[...], g_ref[...]) * (1.0 + sc_ref[...]) + sh_ref[...]).astype(BF16)

    off = [0]

    def section(width):
        z = _dot(h, win_ref[:, off[0]:off[0] + width])
        off[0] += width
        return z

    z_q = section(q_rank)
    z_kv = section(kv_rank)
    z_pe = section(2 * LANES)
    c_tab, s_tab = _rope_cs(cos_ref[...], sin_ref[...], dm["pe_off"], rope)

    qn = _rms(z_q, gqa_ref[...]).astype(BF16)
    zq = _dot(qn, wq_ref[...])
    zqr = _dot(qn, wqr_ref[...])
    gq, gqr, gkn, gkp = gq_ref[...], gqr_ref[...], gkn_ref[...], gkp_ref[...]
    qscale = qk ** -0.5 * LOG2E
    qg = []
    for hh in range(heads):
        sl = slice(hh * LANES, (hh + 1) * LANES)
        a, b = zq[:, sl] * gq, zqr[:, sl] * gqr
        r = lax.rsqrt(jnp.sum(zq[:, sl] * zq[:, sl], axis=-1, keepdims=True) * (1.0 / qk) + EPS) * qscale
        q_fin = (a * c_tab + b * s_tab) * r
        q_rot = (a * s_tab - b * c_tab) * r
        qc_ref[:, sl] = q_fin * gkp
        qs_ref[:, sl] = q_rot * gkp
        qg.append(q_fin * gkn)
    qg = jnp.concatenate(qg, axis=-1)
    g_hi = qg.astype(BF16)
    g_lo = (qg - g_hi.astype(F32)).astype(BF16)
    qabs_ref[...] = _dot(g_hi, bdk_ref[...]) + _dot(g_lo, bdk_ref[...])

    cnew_ref[...] = _rms(z_kv, gkva_ref[...])
    kpe_ref[...] = z_pe[:, :rope]

    z_fq = section(fheads * LANES)
    z_fk = section(fheads * LANES)
    fv_ref[...] = section(fheads * fdim)
    logf_ref[...] = _log_sigmoid(section(LANES) + bf_ref[...])
    gfq, gfk = gfq_ref[...], gfk_ref[...]
    fscale = fdim ** -0.5 * LOG2E
    for hh in range(fheads):
        sl = slice(hh * LANES, (hh + 1) * LANES)
        a = z_fq[:, sl]
        r = lax.rsqrt(jnp.sum(a * a, axis=-1, keepdims=True) * (1.0 / fdim) + EPS)
        fq_ref[:, sl] = (a * gfq) * (r * fscale)
        a = z_fk[:, sl]
        r = lax.rsqrt(jnp.sum(a * a, axis=-1, keepdims=True) * (1.0 / fdim) + EPS)
        fk_ref[:, sl] = (a * gfk) * r


def _attn_kernel(q_ref, k_ref, vt_ref, o_ref, *, dv):
    t = q_ref.shape[1]
    i = pl.program_id(2)
    heads_per_step = q_ref.shape[2] // LANES
    outs = []
    for hh in range(heads_per_step):
        sl = slice(hh * LANES, (hh + 1) * LANES)
        vsl = slice(hh * dv, (hh + 1) * dv)
        q = q_ref[0, :, sl]

        def step(j, carry, masked):
            m, l, acc = carry
            ks = pl.multiple_of(j * t, t)
            s = _dot_nt(k_ref[0, pl.ds(ks, t), sl], q)
            if masked:
                kp = lax.broadcasted_iota(jnp.int32, (t, t), 0)
                qp = lax.broadcasted_iota(jnp.int32, (t, t), 1)
                s = jnp.where(kp <= qp, s, NEG)
            m_new = jnp.maximum(m, jnp.max(s, axis=0, keepdims=True))
            alpha = jnp.exp2(m - m_new)
            p = jnp.exp2(s - m_new)
            l = alpha * l + jnp.sum(p, axis=0, keepdims=True)
            acc = alpha * acc + _dot(vt_ref[0, j, vsl, :], p.astype(BF16))
            return m_new, l, acc

        init = (jnp.full((1, t), NEG, F32), jnp.zeros((1, t), F32), jnp.zeros((dv, t), F32))
        carry = lax.fori_loop(0, i, functools.partial(step, masked=False), init)
        _, l, acc = step(i, carry, True)
        outs.append(acc / l)
    o_ref[0] = jnp.concatenate(outs, axis=0).T


def _attention(q, k, vt, dv):
    bsz, s, hl = q.shape
    t = vt.shape[3]
    hps = LANES // dv
    n_hp = hl // (hps * LANES)
    return pl.pallas_call(
        functools.partial(_attn_kernel, dv=dv),
        grid=(bsz, n_hp, s // t),
        in_specs=[pl.BlockSpec((1, t, hps * LANES), lambda b, hp, i: (b, i, hp)),
                  pl.BlockSpec((1, s, hps * LANES), lambda b, hp, i: (b, 0, hp)),
                  pl.BlockSpec((1, s // t, hps * dv, t), lambda b, hp, i: (b, 0, hp, 0))],
        out_specs=pl.BlockSpec((1, t, hps * dv), lambda b, hp, i: (b, i, hp)),
        out_shape=jax.ShapeDtypeStruct((bsz, s, n_hp * hps * dv), F32),
        compiler_params=_params(("arbitrary", "arbitrary", "arbitrary")),
        name="attention",
    )(q, k, vt)


def _merge_kernel(*refs, absorbed_values):
    if absorbed_values:
        bdv_ref, refs = refs[0], refs[1:]
    (x_ref, om_ref, of_ref, gt2_ref, gom_ref, gof_ref, wom_ref, wof_ref,
     sh_ref, sc_ref, gt_ref, g_ref, wg_ref, wu_ref, wd_ref, o_ref) = refs
    om = om_ref[0]
    if absorbed_values:
        om = _dot(om.astype(BF16), bdv_ref[...])
    mix = (_dot(_rms(om, gom_ref[...]).astype(BF16), wom_ref[...])
           + _dot(_rms(of_ref[0], gof_ref[...]).astype(BF16), wof_ref[...]))
    x = x_ref[0] + gt2_ref[0] * mix
    o_ref[0] = _ffn_core(x, sh_ref[0], sc_ref[0], gt_ref[0], g_ref[...], wg_ref, wu_ref, wd_ref)


def _merge_ffn(x, om, of, gt2, gom, gof, wom, wof, sh, sc, gt, g, wg, wu, wd, tm, bdv=None):
    bsz, s, d = x.shape
    tok = pl.BlockSpec((1, tm, d), lambda b, i: (b, i, 0))

    def tokw(a):
        return pl.BlockSpec((1, tm, a.shape[2]), lambda b, i: (b, i, 0))

    ins = [x, om, of, gt2, gom, gof, wom, wof, sh, sc, gt, g, wg, wu, wd]
    specs = [tok, tokw(om), tokw(of), _mod_spec(gt2, tm, d), _resident(gom.shape), _resident(gof.shape),
             _resident(wom.shape), _resident(wof.shape), _mod_spec(sh, tm, d), _mod_spec(sc, tm, d),
             _mod_spec(gt, tm, d), _resident(g.shape), _resident(wg.shape), _resident(wu.shape),
             _resident(wd.shape)]
    if bdv is not None:
        ins, specs = [bdv] + ins, [_resident(bdv.shape)] + specs
    return pl.pallas_call(
        functools.partial(_merge_kernel, absorbed_values=bdv is not None),
        grid=(bsz, s // tm),
        in_specs=specs,
        out_specs=tok,
        out_shape=jax.ShapeDtypeStruct(x.shape, F32),
        compiler_params=_params(("arbitrary", "arbitrary")),
        name="merge_ffn",
    )(*ins)


def _prefix_kernel(x_ref, t_ref, o_ref):
    hi, mid, lo = _split3(x_ref[...])
    t = t_ref[...]
    y = _dot(hi, t) + _dot(mid, t) + _dot(lo, t)
    for r in range(SUBLANES + 1):
        o_ref[:, r, :] = y[:, r * LANES:(r + 1) * LANES]
    o_ref[:, SUBLANES + 1:, :] = jnp.zeros((o_ref.shape[0], o_ref.shape[1] - SUBLANES - 1, LANES), F32)


def _page_prefix(logf_flat, tmat):
    n_pool, width = logf_flat.shape
    pg = math.gcd(n_pool, 512)
    return pl.pallas_call(
        _prefix_kernel,
        grid=(n_pool // pg,),
        in_specs=[pl.BlockSpec((pg, width), lambda i: (i, 0)), _resident(tmat.shape)],
        out_specs=pl.BlockSpec((pg, 2 * SUBLANES, LANES), lambda i: (i, 0, 0)),
        out_shape=jax.ShapeDtypeStruct((n_pool, 2 * SUBLANES, LANES), F32),
        compiler_params=_params(("arbitrary",)),
        name="page_prefix",
    )(logf_flat, tmat)


def _decode_kernel(pt_ref, *refs, G, dm):
    del pt_ref
    ckv_refs, kpe_refs = refs[0:G], refs[G:2 * G]
    fk_refs, fv_refs, wd_refs = refs[2 * G:3 * G], refs[3 * G:4 * G], refs[4 * G:5 * G]
    (cc_ref, ss_ref, qabs_ref, qc_ref, qs_ref, fq_ref, cnew_ref, kpenew_ref, fknew_ref, fvnew_ref,
     lfnew_ref, ccn_ref, ssn_ref, wk_ref, ind_ref,
     olat_ref, ofox_ref,
     m_m, l_m, acc_m, m_f, l_f, acc_f, carry_scr, s_scr, bd_scr) = refs[5 * G:]
    heads, fheads, rope, qk, fdim, page = dm["heads"], dm["fheads"], dm["rope"], dm["qk"], dm["fdim"], dm["page"]
    gi = pl.program_id(1)
    n_gi = pl.num_programs(1)

    @pl.when(gi == 0)
    def _():
        m_m[...] = jnp.full(m_m.shape, NEG, F32)
        l_m[...] = jnp.zeros_like(l_m)
        acc_m[...] = jnp.zeros_like(acc_m)
        m_f[...] = jnp.full(m_f.shape, NEG, F32)
        l_f[...] = jnp.zeros_like(l_f)
        acc_f[...] = jnp.zeros_like(acc_f)
        carry_scr[...] = jnp.zeros_like(carry_scr)

    qabs = qabs_ref[0].astype(BF16)
    qc = qc_ref[0][:, :rope].astype(BF16)
    qs = qs_ref[0][:, :rope].astype(BF16)
    eye = (lax.broadcasted_iota(jnp.int32, (heads, heads), 0)
           == lax.broadcasted_iota(jnp.int32, (heads, heads), 1))

    def to_col(row):
        return jnp.sum(jnp.where(eye, jnp.broadcast_to(row, (heads, heads)), 0.0), axis=-1, keepdims=True)

    def mla_update(ckv, kpe, cc, ss, valid_rows):
        ckv_b = ckv.astype(BF16)
        kn = _dot(ckv_b, wk_ref[...])
        ssq = _dot((kn * kn).astype(BF16), ind_ref[...])
        ssq = ssq + jnp.sum(kpe * kpe, axis=-1, keepdims=True)
        r = lax.rsqrt(ssq * (1.0 / qk) + EPS)
        s = _dot_nt(ckv_b, qabs)
        s = s + _dot_nt((kpe * cc).astype(BF16), qc) + _dot_nt((kpe * ss).astype(BF16), qs)
        s = s * r
        if valid_rows is not None:
            row = lax.broadcasted_iota(jnp.int32, s.shape, 0)
            s = jnp.where(row < valid_rows, s, NEG)
        m_old = m_m[...]
        m_new = jnp.maximum(m_old, jnp.max(s, axis=0, keepdims=True))
        alpha = jnp.exp2(m_old - m_new)
        p = jnp.exp2(s - m_new)
        m_m[...] = m_new
        l_m[...] = alpha * l_m[...] + jnp.sum(p, axis=0, keepdims=True)
        pv = lax.dot_general(p.astype(BF16), ckv_b, _TN, preferred_element_type=F32)
        acc_m[...] = to_col(alpha) * acc_m[...] + pv

    ckv_all = jnp.concatenate([r[0, 0] for r in ckv_refs], axis=0)
    kpe_all = jnp.concatenate([r[0, 0] for r in kpe_refs], axis=0)
    mla_update(ckv_all, kpe_all, cc_ref[...], ss_ref[...], None)

    fq = fq_ref[0][:, :fdim]
    lane = lax.broadcasted_iota(jnp.int32, (fheads, fdim), 1)
    sub = lax.broadcasted_iota(jnp.int32, (fheads, fdim), 0)
    diag = lane - sub
    per_row = LANES // fheads
    slots = fdim // fheads

    def logit(kvec, bias_row, slot):
        t = kvec * fq + jnp.where(diag == slot * fheads, jnp.broadcast_to(bias_row, (fheads, LANES))[:, :fdim], 0.0)
        return jnp.broadcast_to(jnp.sum(t, axis=-1, keepdims=True), (fheads, fdim))

    m_run = jnp.full((fheads, fdim), NEG, F32)
    for g in range(G):
        wd = wd_refs[g][0]
        carry = carry_scr[...]
        bd = (wd[0:SUBLANES] + carry) * (-LOG2E)
        bd_scr[0:SUBLANES] = bd
        bd_scr[SUBLANES:2 * SUBLANES] = pltpu.roll(bd, LANES - fdim, axis=1)
        carry_scr[...] = carry + wd[SUBLANES:SUBLANES + 1]

        def pass1(p, m_acc, g=g):
            r = lax.shift_right_logical(p, int(math.log2(per_row)))
            j = p & (per_row - 1)
            half = lax.shift_right_logical(j, int(math.log2(slots)))
            row = bd_scr[pl.ds(r + SUBLANES * half, 1), :]
            sb = logit(fk_refs[g][0, 0, p], row, j & (slots - 1))
            s_scr[g * page + p] = sb
            return jnp.maximum(m_acc, sb)

        m_run = lax.fori_loop(0, page, pass1, m_run, unroll=8)

    def fox_accumulate(m_new):
        m_old = m_f[...]
        alpha = jnp.exp2(m_old - m_new)
        m_f[...] = m_new
        return l_f[...] * alpha, acc_f[...] * alpha

    m_new = jnp.maximum(m_f[...], m_run)
    l_acc, acc = fox_accumulate(m_new)
    for g in range(G):
        def pass2(p, c, g=g):
            l_c, a_c = c
            pp = jnp.exp2(s_scr[g * page + p] - m_new)
            return l_c + pp, a_c + pp * fv_refs[g][0, 0, p]

        l_acc, acc = lax.fori_loop(0, page, pass2, (l_acc, acc), unroll=8)
    l_f[...] = l_acc
    acc_f[...] = acc

    @pl.when(gi == n_gi - 1)
    def _():
        rows = SUBLANES
        mla_update(jnp.broadcast_to(cnew_ref[0], (rows, cnew_ref.shape[2])),
                   jnp.broadcast_to(kpenew_ref[0], (rows, rope)),
                   jnp.broadcast_to(ccn_ref[...], (rows, rope)),
                   jnp.broadcast_to(ssn_ref[...], (rows, rope)), 1)
        olat_ref[0] = acc_m[...] / to_col(l_m[...])

        bias_row = (carry_scr[...] + lfnew_ref[0]) * (-LOG2E)
        sb = logit(fknew_ref[0][:, :fdim], bias_row, 0)
        m_fin = jnp.maximum(m_f[...], sb)
        l_c, a_c = fox_accumulate(m_fin)
        pp = jnp.exp2(sb - m_fin)
        ofox_ref[0] = (a_c + pp * fvnew_ref[0]) / (l_c + pp)


def _decode(layer, page_table, cache_ckv, cache_kpe, cache_fk, cache_fv, wpre, cc, ss, qabs, qc, qs, fq,
            cnew, kpenew, fknew, fvnew, lfnew, ccn, ssn, wk, ind, dm, G):
    db, n_pages = page_table.shape
    page, heads, fheads, fdim, rope = dm["page"], dm["heads"], dm["fheads"], dm["fdim"], dm["rope"]
    kv_rank = cache_ckv.shape[-1]
    assert n_pages % G == 0

    def paged(block, g, lead):
        nz = len(block) - len(lead) - 1
        return pl.BlockSpec(block, lambda b, gi, pt, g=g: (*lead, pt[b, gi * G + g]) + (0,) * nz)

    def per_seq(a):
        nz = a.ndim - 1
        return pl.BlockSpec((1,) + a.shape[1:], lambda b, gi, pt: (b,) + (0,) * nz)

    def whole(a):
        nd = a.ndim
        return pl.BlockSpec(a.shape, lambda b, gi, pt: (0,) * nd)

    in_specs = ([paged((1, 1, page, kv_rank), g, (layer,)) for g in range(G)]
                + [paged((1, 1, page, rope), g, (layer,)) for g in range(G)]
                + [paged((1, 1, page, fheads, fdim), g, (layer,)) for g in range(G)]
                + [paged((1, 1, page, fheads, fdim), g, (layer,)) for g in range(G)]
                + [paged((1, 2 * SUBLANES, LANES), g, ()) for g in range(G)]
                + [pl.BlockSpec((G * page, rope), lambda b, gi, pt: (gi, 0))] * 2
                + [per_seq(a) for a in (qabs, qc, qs, fq, cnew, kpenew, fknew, fvnew, lfnew)]
                + [whole(a) for a in (ccn, ssn, wk, ind)])
    operands = ([cache_ckv] * G + [cache_kpe] * G + [cache_fk] * G + [cache_fv] * G + [wpre] * G
                + [cc, ss, qabs, qc, qs, fq, cnew, kpenew, fknew, fvnew, lfnew, ccn, ssn, wk, ind])
    grid_spec = pltpu.PrefetchScalarGridSpec(
        num_scalar_prefetch=1,
        grid=(db, n_pages // G),
        in_specs=in_specs,
        out_specs=[pl.BlockSpec((1, heads, kv_rank), lambda b, gi, pt: (b, 0, 0)),
                   pl.BlockSpec((1, fheads, fdim), lambda b, gi, pt: (b, 0, 0))],
        scratch_shapes=[pltpu.VMEM((1, heads), F32), pltpu.VMEM((1, heads), F32),
                        pltpu.VMEM((heads, kv_rank), F32),
                        pltpu.VMEM((fheads, fdim), F32), pltpu.VMEM((fheads, fdim), F32),
                        pltpu.VMEM((fheads, fdim), F32), pltpu.VMEM((1, LANES), F32),
                        pltpu.VMEM((G * page, fheads, fdim), F32), pltpu.VMEM((2 * SUBLANES, LANES), F32)])
    return pl.pallas_call(
        functools.partial(_decode_kernel, G=G, dm=dm),
        grid_spec=grid_spec,
        out_shape=[jax.ShapeDtypeStruct((db, heads, kv_rank), F32),
                   jax.ShapeDtypeStruct((db, fheads, fdim), F32)],
        compiler_params=_params(("arbitrary", "arbitrary")),
        name="decode",
    )(page_table, *operands)


def _slab(nope_part, pe_part, nope_off, pe_off):
    out = jnp.zeros(nope_part.shape[:-1] + (LANES,), nope_part.dtype)
    out = out.at[..., nope_off:nope_off + nope_part.shape[-1]].set(nope_part)
    out = out.at[..., pe_off:pe_off + pe_part.shape[-1]].set(pe_part)
    return out.reshape(out.shape[:-2] + (out.shape[-2] * LANES,))


def _rot_half(x):
    half = x.shape[-1] // 2
    return jnp.concatenate([-x[..., half:], x[..., :half]], axis=-1)


def _swap_half(x):
    half = x.shape[-1] // 2
    return jnp.concatenate([x[..., half:], x[..., :half]], axis=-1)


def _head_slab(x, width):
    pad = [(0, 0)] * (x.ndim - 1) + [(0, LANES - width)]
    out = jnp.pad(x, pad)
    return out.reshape(out.shape[:-2] + (out.shape[-2] * LANES,))


def _w_in_layout(w_in, dm, nope_off, pe_off):
    q_rank, kv_rank, rope, fheads, fdim = dm["q_rank"], dm["kv_rank"], dm["rope"], dm["fheads"], dm["fdim"]
    d = w_in.shape[0]
    fw = fheads * fdim
    o = 0
    w_q = w_in[:, o:o + q_rank]; o += q_rank
    w_kv = w_in[:, o:o + kv_rank]; o += kv_rank
    w_pe = w_in[:, o:o + rope]; o += rope
    w_fq = w_in[:, o:o + fw]; o += fw
    w_fk = w_in[:, o:o + fw]; o += fw
    w_fv = w_in[:, o:o + fw]; o += fw
    w_f = w_in[:, o:o + fheads]
    pe_a = jnp.zeros((d, LANES), w_in.dtype).at[:, pe_off:pe_off + rope].set(w_pe)
    pe_b = jnp.zeros((d, LANES), w_in.dtype).at[:, pe_off:pe_off + rope].set(_rot_half(w_pe))
    if pe_off != 0:
        pe_b = pe_b.at[:, :rope].set(w_pe)
    return jnp.concatenate([w_q, w_kv, pe_a, pe_b,
                            _head_slab(w_fq.reshape(d, fheads, fdim), fdim),
                            _head_slab(w_fk.reshape(d, fheads, fdim), fdim),
                            w_fv, jnp.tile(w_f, (1, LANES // fheads))], axis=1).astype(BF16), w_fv


def _row(x):
    return x.reshape(1, -1).astype(F32)


def _chunk_ffn(w_gate, w_up, w_down):
    d, ff = w_gate.shape
    fc = MXU_DIM
    n = ff // fc
    return (w_gate.reshape(d, n, fc).transpose(1, 0, 2).astype(BF16),
            w_up.reshape(d, n, fc).transpose(1, 0, 2).astype(BF16),
            w_down.reshape(n, fc, d).astype(BF16))


PROMPT_TILE = 512
DECODE_PAGES_PER_STEP = 4


def kernel(x_prompt, x_sample, cache_mla_ckv, cache_mla_kpe, cache_fox_k, cache_fox_v, cache_fox_logf, page_table, c_prompt, c_sample, w_ada, b_ada, g_ffn1, w1_gate, w1_up, w1_down, g_mix, w_in, g_q_a, w_q_b, g_kv_a, w_kv_b, g_mla_q, g_mla_k, b_fox_f, g_fox_q, g_fox_k, g_out_mla, g_out_fox, w_o, g_ffn2, w2_gate, w2_up, w2_down):
    bsz, seq, d = x_prompt.shape
    db, dec_seq, _ = x_sample.shape
    assert dec_seq == 1, "decode path handles one new token per sequence"
    depth = w_in.shape[0]
    n_pool, page = cache_mla_ckv.shape[1], cache_mla_ckv.shape[2]
    n_pages = page_table.shape[1]
    past_len = n_pages * page
    q_rank, heads, qk = w_q_b.shape[1], w_q_b.shape[2], w_q_b.shape[3]
    kv_rank = w_kv_b.shape[1]
    rope = cache_mla_kpe.shape[-1]
    nope = qk - rope
    vdim = w_kv_b.shape[3] - nope
    fheads, fdim = cache_fox_k.shape[3], cache_fox_k.shape[4]
    half = rope // 2
    assert LANES % half == 0 and qk <= LANES and fdim + 6 <= LANES and fheads == SUBLANES
    assert vdim == fdim and LANES == 2 * fdim and fdim % fheads == 0
    tm = min(PROMPT_TILE, seq)
    assert seq % tm == 0 and (fheads * page) % LANES == 0
    dm = dict(q_rank=q_rank, kv_rank=kv_rank, heads=heads, fheads=fheads, rope=rope, qk=qk, fdim=fdim,
              page=page)
    dm_p = dict(dm, pe_off=nope)
    dm_s = dict(dm, pe_off=0)

    cos_t, sin_t = _rope_tables(max(seq, past_len + dec_seq), half)
    cc, ss = cos_t[:past_len, :rope], sin_t[:past_len, :rope]
    cos_new = jnp.broadcast_to(cos_t[past_len:past_len + 1], (db, LANES))
    sin_new = jnp.broadcast_to(sin_t[past_len:past_len + 1], (db, LANES))
    ccn, ssn = cos_t[past_len:past_len + 1, :rope], sin_t[past_len:past_len + 1, :rope]

    tri = (jnp.arange(tm)[:, None] >= jnp.arange(tm)[None, :]).astype(BF16)
    hidx = jnp.arange(fheads)
    place = jnp.zeros((LANES, 2 * fheads * LANES), F32)
    for part in range(3):
        place = place.at[part * fheads + hidx, hidx * LANES + fdim + part].set(1.0)
        place = place.at[part * fheads + hidx, fheads * LANES + hidx * LANES + fdim + 3 + part].set(-1.0)
    place = place.astype(BF16)
    slab_lane = jnp.arange(fheads * LANES) % LANES
    qones = ((slab_lane >= fdim + 3) & (slab_lane < fdim + 6)).astype(F32).reshape(1, -1)
    kones = ((slab_lane >= fdim) & (slab_lane < fdim + 3)).astype(F32).reshape(1, -1)

    flat = jnp.arange(page * fheads)
    same_head = (flat[:, None] % fheads) == (flat[None, :] % fheads)
    tmat = jnp.concatenate(
        [(same_head & (flat[:, None] <= flat[None, :])),
         (flat[:, None] % fheads) == (jnp.arange(LANES)[None, :] % fheads)], axis=1).astype(BF16)
    ind = (jnp.arange(heads * nope)[:, None] // nope == jnp.arange(heads)[None, :]).astype(BF16)

    yp = x_prompt
    ys = x_sample.reshape(1, db, d)
    prompt_states, sample_states = [], []
    for l in range(depth):
        mod = _adaln(jnp.concatenate([c_prompt, c_sample], axis=0), w_ada[l], b_ada[l])
        mp = [m.reshape(bsz, 1, d) for m in jnp.split(mod[:bsz], 9, axis=-1)]
        ms = [m.reshape(1, db, d) for m in jnp.split(mod[bsz:], 9, axis=-1)]
        ffn1 = _chunk_ffn(w1_gate[l], w1_up[l], w1_down[l])
        ffn2 = _chunk_ffn(w2_gate[l], w2_up[l], w2_down[l])

        wq_nope, wq_pe = w_q_b[l][..., :nope], w_q_b[l][..., nope:]
        gq_n, gq_p = g_mla_q[l][:nope], g_mla_q[l][nope:]
        gk_n, gk_p = g_mla_k[l][:nope], g_mla_k[l][nope:]
        zeros_n, zeros_p = jnp.zeros_like(gq_n), jnp.zeros_like(gq_p)
        wk_nope = w_kv_b[l][..., :nope]
        wv = w_kv_b[l][..., nope:]

        def mla_layout(nope_off, pe_off):
            wq = _slab(wq_nope, wq_pe, nope_off, pe_off).astype(BF16)
            wqr = _slab(jnp.zeros_like(wq_nope), _rot_half(wq_pe), nope_off, pe_off).astype(BF16)
            gq = _slab(gq_n[None], gq_p[None], nope_off, pe_off)
            gqr = _slab(zeros_n[None], _swap_half(gq_p)[None], nope_off, pe_off)
            return wq, wqr, gq, gqr

        win_p, w_fv = _w_in_layout(w_in[l], dm, 0, nope)
        win_s, _ = _w_in_layout(w_in[l], dm, rope, 0)
        wq_p, wqr_p, gq_ps, gqr_ps = mla_layout(0, nope)
        wq_s, wqr_s, gq_ss, gqr_ss = mla_layout(rope, 0)
        gk_ps = _slab(gk_n[None], gk_p[None], 0, nope)
        gkr_ps = _slab(zeros_n[None], _swap_half(gk_p)[None], 0, nope)
        gkn_ss = _slab(gk_n[None], zeros_p[None], rope, 0)
        gkp_ss = _slab(zeros_n[None], gk_p[None], rope, 0)
        wk_pad = _slab(wk_nope, jnp.zeros(wk_nope.shape[:-1] + (rope,), F32), 0, nope).astype(BF16)
        wk_flat = wk_nope.reshape(kv_rank, heads * nope).astype(BF16)
        wvt = wv.reshape(kv_rank, heads * vdim).T.astype(BF16)
        wfvt = w_fv.T.astype(BF16)
        bdk = jnp.zeros((heads, LANES, heads, kv_rank), F32)
        bdk = bdk.at[jnp.arange(heads), rope:rope + nope, jnp.arange(heads), :].set(
            wk_nope.transpose(1, 2, 0)).reshape(heads * LANES, heads * kv_rank).astype(BF16)
        bdv = jnp.zeros((heads, kv_rank, heads, vdim), F32)
        bdv = bdv.at[jnp.arange(heads), :, jnp.arange(heads), :].set(
            wv.transpose(1, 0, 2)).reshape(heads * kv_rank, heads * vdim).astype(BF16)
        gfq_s = jnp.pad(g_fox_q[l], (0, LANES - fdim)).reshape(1, LANES)
        gfk_s = jnp.pad(g_fox_k[l], (0, LANES - fdim)).reshape(1, LANES)
        bf_pat = jnp.tile(b_fox_f[l], LANES // fheads).reshape(1, LANES)
        wom, wof = w_o[l][:heads * vdim].astype(BF16), w_o[l][heads * vdim:].astype(BF16)

        x1 = _ffn(yp, mp[0], mp[1], mp[2], _row(g_ffn1[l]), *ffn1, tm)
        n_t = seq // tm
        tokb = lambda w, dt: jax.ShapeDtypeStruct((bsz, seq, w), dt)
        vt_shape = jax.ShapeDtypeStruct((bsz, n_t, heads * vdim, tm), BF16)
        tok = lambda w: pl.BlockSpec((1, tm, w), lambda b, i: (b, i, 0))
        vt_spec = pl.BlockSpec((1, 1, heads * vdim, tm), lambda b, i: (b, i, 0, 0))
        res_in = [_row(g_mix[l]), win_p, _row(g_q_a[l]), wq_p, wqr_p, _row(g_kv_a[l]), wk_pad, wvt, wfvt,
                  gq_ps, gqr_ps, gk_ps, gkr_ps, gfq_s, gfk_s, bf_pat]
        res_tail = [tri, place, qones, kones]
        (qm, km, vtm, qf, kf, vtf, p_ckv, p_kpe, p_fk, p_fv, p_logf) = pl.pallas_call(
            functools.partial(_proj_prompt_kernel, dm=dm_p),
            grid=(bsz, n_t),
            in_specs=([tok(d), _mod_spec(mp[3], tm, d), _mod_spec(mp[4], tm, d)]
                      + [_resident(a.shape) for a in res_in]
                      + [pl.BlockSpec((tm, LANES), lambda b, i: (i, 0))] * 2
                      + [_resident(a.shape) for a in res_tail]),
            out_specs=[tok(heads * LANES), tok(heads * LANES), vt_spec, tok(fheads * LANES),
                       tok(fheads * LANES), vt_spec, tok(kv_rank), tok(rope), tok(fheads * LANES),
                       tok(fheads * fdim), tok(fheads)],
            out_shape=[tokb(heads * LANES, BF16), tokb(heads * LANES, BF16), vt_shape,
                       tokb(fheads * LANES, BF16), tokb(fheads * LANES, BF16), vt_shape,
                       tokb(kv_rank, F32), tokb(rope, F32), tokb(fheads * LANES, F32),
                       tokb(fheads * fdim, F32), tokb(fheads, F32)],
            scratch_shapes=[pltpu.VMEM((1, LANES), F32)],
            compiler_params=_params(("arbitrary", "arbitrary")),
            name="proj_prompt",
        )(x1, mp[3], mp[4], *res_in, cos_t, sin_t, *res_tail)
        o_mla = _attention(qm, km, vtm, vdim)
        o_fox = _attention(qf, kf, vtf, fdim)
        yp = _merge_ffn(x1, o_mla, o_fox, mp[5], _row(g_out_mla[l]), _row(g_out_fox[l]), wom, wof,
                        mp[6], mp[7], mp[8], _row(g_ffn2[l]), *ffn2, tm)
        prompt_states.append((p_ckv, p_kpe,
                              p_fk.reshape(bsz, seq, fheads, LANES)[..., :fdim],
                              p_fv.reshape(bsz, seq, fheads, fdim), p_logf))

        xs1 = _ffn(ys, ms[0], ms[1], ms[2], _row(g_ffn1[l]), *ffn1, db)
        res_s = [_row(g_mix[l]), win_s, _row(g_q_a[l]), wq_s, wqr_s, _row(g_kv_a[l]),
                 gq_ss, gqr_ss, gkn_ss, gkp_ss, gfq_s, gfk_s, bf_pat, cos_new, sin_new, bdk]
        sds = lambda w: jax.ShapeDtypeStruct((db, w), F32)
        (qabs, qc, qs, c_new, kpe_new, fq_s, fk_s, fv_s, logf_s) = pl.pallas_call(
            functools.partial(_proj_sample_kernel, dm=dm_s),
            out_shape=[sds(heads * kv_rank), sds(heads * LANES), sds(heads * LANES), sds(kv_rank), sds(rope),
                       sds(fheads * LANES), sds(fheads * LANES), sds(fheads * fdim), sds(LANES)],
            compiler_params=pltpu.CompilerParams(vmem_limit_bytes=VMEM_LIMIT_BYTES),
            name="proj_sample",
        )(xs1[0], ms[3][0], ms[4][0], *res_s)

        wpre = _page_prefix(cache_fox_logf[l].reshape(n_pool, page * fheads), tmat)
        o_lat, o_fx = _decode(
            l, page_table, cache_mla_ckv, cache_mla_kpe, cache_fox_k, cache_fox_v, wpre, cc, ss,
            qabs.reshape(db, heads, kv_rank), qc.reshape(db, heads, LANES), qs.reshape(db, heads, LANES),
            fq_s.reshape(db, fheads, LANES), c_new.reshape(db, 1, kv_rank), kpe_new.reshape(db, 1, rope),
            fk_s.reshape(db, fheads, LANES), fv_s.reshape(db, fheads, fdim), logf_s.reshape(db, 1, LANES),
            ccn, ssn, wk_flat, ind, dm, math.gcd(DECODE_PAGES_PER_STEP, n_pages))
        ys = _merge_ffn(xs1, o_lat.reshape(1, db, heads * kv_rank), o_fx.reshape(1, db, fheads * fdim), ms[5],
                        _row(g_out_mla[l]), _row(g_out_fox[l]), wom, wof,
                        ms[6], ms[7], ms[8], _row(g_ffn2[l]), *ffn2, db, bdv=bdv)
        sample_states.append((c_new.reshape(db, 1, kv_rank), kpe_new.reshape(db, 1, rope),
                              fk_s.reshape(db, 1, fheads, LANES)[..., :fdim],
                              fv_s.reshape(db, 1, fheads, fdim), logf_s[:, :fheads].reshape(db, 1, fheads)))

    p_out = [jnp.stack(z) for z in zip(*prompt_states)]
    s_out = [jnp.stack(z) for z in zip(*sample_states)]
    return (yp, ys.reshape(db, dec_seq, d), *p_out, *s_out)
```
